```python
import math
import jax, jax.numpy as jnp
from jax import lax
import numpy as np

D_MODEL = 1024
BATCH = 2
SEQ = 16384
DEPTH = 1
DEC_BATCH = 128
DEC_SEQ = 8
PAST_LEN = 8192
PAGE_SIZE = 128

D_MIX = D_MODEL
GLA_HEADS = 4
GLA_DK = D_MIX // 16
GLA_DV = D_MIX // 8
GLA_WIDTH = GLA_HEADS * GLA_DV
GLA_RANK = 16
GLA_TAU = 16.0
GLA_CHUNK = 64
MOBA_HEADS = 8
MOBA_HD = D_MIX // 16
MOBA_WIDTH = MOBA_HEADS * MOBA_HD
MOBA_BLOCK = 256
MOBA_TOPK = 3
MOBA_QBLOCK = 64
ROPE_THETA = 500000.0
ROT_DIM = MOBA_HD // 4
MEM_TOKENS = 256
MEM_HEADS = 4
MEM_HD = D_MODEL // 8
MEM_WIDTH = MEM_HEADS * MEM_HD
N_GROUPS = 4
EXPERTS_PER_GROUP = 4
N_EXPERTS = N_GROUPS * EXPERTS_PER_GROUP
TOP_K_INNER = 2
D_EXPERT = D_MODEL // 2
EPS = 1e-6
IN_SIZES = (GLA_HEADS * GLA_DK, GLA_HEADS * GLA_DK, GLA_WIDTH, GLA_WIDTH, GLA_RANK, MOBA_WIDTH, MOBA_WIDTH, MOBA_WIDTH)
D_IN = 2 * GLA_HEADS * GLA_DK + 2 * GLA_WIDTH + GLA_RANK + 3 * MOBA_WIDTH

kernel_name = 'hymba_gla_moba_hmoe_step'


def rmsnorm(x, g):
    xf = x.astype(jnp.float32)
    y = xf * lax.rsqrt(jnp.mean(xf * xf, axis=-1, keepdims=True) + EPS)
    return (y * g.astype(jnp.float32)).astype(x.dtype)


def partial_rope(x, pos):
    half = ROT_DIM // 2
    inv = ROPE_THETA ** (-jnp.arange(half, dtype=jnp.float32) * 2.0 / ROT_DIM)
    ang = pos.astype(jnp.float32)[:, None] * inv[None, :]
    cos = jnp.cos(ang)[:, None, :]
    sin = jnp.sin(ang)[:, None, :]
    xf = x.astype(jnp.float32)
    x1 = xf[..., :half]
    x2 = xf[..., half:ROT_DIM]
    out = jnp.concatenate([x1 * cos - x2 * sin, x2 * cos + x1 * sin, xf[..., ROT_DIM:]], axis=-1)
    return out.astype(x.dtype)


def gla_chunked(q, k, v, log_a, s0, chunk):
    B, L, H, DK = q.shape
    DV = v.shape[-1]
    nc = L // chunk

    def to_chunks(t):
        return t.reshape(B, nc, chunk, H, t.shape[-1]).transpose(1, 0, 3, 2, 4)

    causal = jnp.tril(jnp.ones((chunk, chunk), dtype=bool))

    def step(S, xs):
        qc, kc, vc, ac = xs
        b = jnp.cumsum(ac, axis=-2)
        b_last = b[..., -1:, :]
        q_dec = qc * jnp.exp(b)
        attn = jnp.einsum('bhcd,bhsd->bhcs', q_dec, kc * jnp.exp(-b))
        attn = jnp.where(causal, attn, 0.0)
        o = jnp.einsum('bhcs,bhsv->bhcv', attn, vc) + jnp.einsum('bhcd,bhdv->bhcv', q_dec, S)
        S_new = jnp.exp(b_last)[..., 0, :, None] * S + jnp.einsum('bhsd,bhsv->bhdv', kc * jnp.exp(b_last - b), vc)
        return S_new, o

    S_fin, o = lax.scan(step, s0, (to_chunks(q), to_chunks(k), to_chunks(v), to_chunks(log_a)))
    o = o.transpose(1, 0, 3, 2, 4).reshape(B, L, H, DV)
    return o, S_fin


def moba_attend(q, q_pos, blk_mean, fetch, q_block):
    B, L, H, D = q.shape
    nbs = blk_mean.shape[1]
    nq = L // q_block
    nk = (MOBA_TOPK + 1) * MOBA_BLOCK
    r = jnp.arange(MOBA_BLOCK, dtype=jnp.int32)
    blk_ids = jnp.arange(nbs, dtype=jnp.int32)
    ranks = jnp.arange(MOBA_TOPK, dtype=jnp.int32)
    means = blk_mean.astype(jnp.float32)
    scale = D ** -0.5

    def one_block(args):
        qb, pb = args
        qf = qb.astype(jnp.float32)
        own = pb // MOBA_BLOCK
        gate = jnp.einsum('bqhd,bjhd->bqhj', qf, means)
        gate = jnp.where(blk_ids < own[:, None, None], gate, -jnp.inf)
        _, sel = lax.top_k(gate, MOBA_TOPK)
        sel_pos = sel[..., None] * MOBA_BLOCK + r
        own_rows = own[:, None] * MOBA_BLOCK + r
        own_pos = jnp.broadcast_to(own_rows[None, :, None, None, :], (B, q_block, H, 1, MOBA_BLOCK))
        pos = jnp.concatenate([sel_pos, own_pos], axis=3).reshape(B, q_block, H, nk)
        sel_ok = jnp.broadcast_to((ranks < own[:, None])[None, :, None, :, None], (B, q_block, H, MOBA_TOPK, MOBA_BLOCK))
        own_ok = jnp.broadcast_to((own_rows <= pb[:, None])[None, :, None, None, :], (B, q_block, H, 1, MOBA_BLOCK))
        valid = jnp.concatenate([sel_ok, own_ok], axis=3).reshape(B, q_block, H, nk)
        k_rows, v_rows = fetch(pos.transpose(0, 1, 3, 2).reshape(B, q_block * nk, H))
        k_rows = k_rows.reshape(B, q_block, nk, H, D).astype(jnp.float32)
        v_rows = v_rows.reshape(B, q_block, nk, H, D).astype(jnp.float32)
        s = jnp.einsum('bqhd,bqkhd->bqhk', qf, k_rows) * scale
        pr = jax.nn.softmax(jnp.where(valid, s, -jnp.inf), axis=-1)
        return jnp.einsum('bqhk,bqkhd->bqhd', pr, v_rows)

    qs = q.reshape(B, nq, q_block, H, D).swapaxes(0, 1)
    ps = q_pos.reshape(nq, q_block)
    out = lax.map(one_block, (qs, ps))
    return out.swapaxes(0, 1).reshape(B, L, H, D)


def prompt_moba(q, k, v, pos):
    B, T, H, D = k.shape
    nbf = T // MOBA_BLOCK
    nbs = max(nbf, MOBA_TOPK)
    means = jnp.mean(k[:, :nbf * MOBA_BLOCK].astype(jnp.float32).reshape(B, nbf, MOBA_BLOCK, H, D), axis=2)
    means = jnp.pad(means, ((0, 0), (0, nbs - nbf), (0, 0), (0, 0)))
    b_idx = jnp.arange(B)[:, None, None]
    h_idx = jnp.arange(H)[None, None, :]

    def fetch(kpos):
        kpos = jnp.clip(kpos, 0, T - 1)
        return k[b_idx, kpos, h_idx], v[b_idx, kpos, h_idx]

    return moba_attend(q, pos, means, fetch, math.gcd(MOBA_QBLOCK, T))


def sample_moba(q, k_new, v_new, pos, cache_k, cache_v, page_table):
    B, L, H, D = k_new.shape
    ps = cache_k.shape[1]
    n_pages = page_table.shape[1]
    past = n_pages * ps
    T = past + L
    nbs = max(T // MOBA_BLOCK, MOBA_TOPK)
    page_sums = jnp.sum(cache_k[page_table], axis=2, dtype=jnp.float32)
    blk = jnp.arange(nbs)
    oh_page = (((jnp.arange(n_pages) * ps) // MOBA_BLOCK)[:, None] == blk[None, :]).astype(jnp.float32)
    oh_new = (((past + jnp.arange(L)) // MOBA_BLOCK)[:, None] == blk[None, :]).astype(jnp.float32)
    means = (jnp.einsum('bphd,pj->bjhd', page_sums, oh_page)
             + jnp.einsum('blhd,lj->bjhd', k_new.astype(jnp.float32), oh_new)) / MOBA_BLOCK
    b_idx = jnp.arange(B)[:, None, None]
    h_idx = jnp.arange(H)[None, None, :]

    def fetch(kpos):
        kpos = jnp.clip(kpos, 0, T - 1)
        pp = jnp.minimum(kpos, past - 1)
        phys = page_table[b_idx, pp // ps]
        off = pp % ps
        pn = jnp.clip(kpos - past, 0, L - 1)
        in_past = (kpos < past)[..., None]
        k_rows = jnp.where(in_past, cache_k[phys, off, h_idx], k_new[b_idx, pn, h_idx])
        v_rows = jnp.where(in_past, cache_v[phys, off, h_idx], v_new[b_idx, pn, h_idx])
        return k_rows, v_rows

    return moba_attend(q, pos, means, fetch, 1)


def mem_kv(mem, g, w_mk, w_mv):
    B, M, _ = mem.shape
    hm = rmsnorm(mem, g)
    k = jnp.einsum('bmd,de->bme', hm, w_mk).reshape(B, M, MEM_HEADS, MEM_HD)
    v = jnp.einsum('bmd,de->bme', hm, w_mv).reshape(B, M, MEM_HEADS, MEM_HD)
    return k, v


def mem_attend(h, mem_k, mem_v, w_mq, w_mo):
    B, L, _ = h.shape
    q = jnp.einsum('bld,de->ble', h, w_mq).reshape(B, L, MEM_HEADS, MEM_HD)
    s = jnp.einsum('blhd,bmhd->blhm', q.astype(jnp.float32), mem_k.astype(jnp.float32)) * MEM_HD ** -0.5
    pr = jax.nn.softmax(s, axis=-1)
    o = jnp.einsum('blhm,bmhd->blhd', pr, mem_v.astype(jnp.float32)).reshape(B, L, MEM_WIDTH)
    return jnp.einsum('ble,ed->bld', o.astype(h.dtype), w_mo)


def hier_moe(h, w_grp, b_grp, w_router, b_router, w_e_gate, w_e_up, w_e_down):
    shp = h.shape
    t = h.reshape(-1, D_MODEL)
    g_logits = (t @ w_grp).astype(jnp.float32) + b_grp.astype(jnp.float32)
    g_prob = jax.nn.softmax(g_logits, axis=-1)
    grp = jnp.argmax(g_logits, axis=-1)
    p_grp = jnp.max(g_prob, axis=-1)
    e_logits = ((t @ w_router).astype(jnp.float32) + b_router.astype(jnp.float32)).reshape(-1, N_GROUPS, EXPERTS_PER_GROUP)
    e_in = jnp.einsum('tge,tg->te', e_logits, jax.nn.one_hot(grp, N_GROUPS, dtype=jnp.float32))
    top_p, top_i = lax.top_k(jax.nn.softmax(e_in, axis=-1), TOP_K_INNER)
    top_p = top_p / jnp.sum(top_p, axis=-1, keepdims=True)
    oh = jax.nn.one_hot(grp[:, None] * EXPERTS_PER_GROUP + top_i, N_EXPERTS, dtype=jnp.float32)
    gates = jnp.einsum('tke,tk->te', oh, p_grp[:, None] * top_p)
    out = jnp.zeros((t.shape[0], D_MODEL), jnp.float32)
    for e in range(N_EXPERTS):
        hid = jax.nn.silu(t @ w_e_gate[e]) * (t @ w_e_up[e])
        out = out + gates[:, e:e + 1] * (hid @ w_e_down[e]).astype(jnp.float32)
    return out.reshape(shp).astype(h.dtype)


def split_in(proj):
    offs = []
    acc = 0
    for s in IN_SIZES[:-1]:
        acc += s
        offs.append(acc)
    return jnp.split(proj, offs, axis=-1)


def layer_forward(x, pos, s0, mem_k, mem_v, attend, p, gla_chunk):
    B, L, _ = x.shape
    f32 = jnp.float32
    h = rmsnorm(x, p['norm_mix'])
    proj = jnp.einsum('bld,de->ble', h, p['w_in'])
    gq, gk, gv, gg, ga, bq, bk, bv = split_in(proj)
    q = gq.reshape(B, L, GLA_HEADS, GLA_DK).astype(f32) * GLA_DK ** -0.5
    k = gk.reshape(B, L, GLA_HEADS, GLA_DK).astype(f32)
    v = gv.reshape(B, L, GLA_HEADS, GLA_DV).astype(f32)
    a_logit = (jnp.einsum('blr,re->ble', ga, p['w_a2']) + p['b_a']).astype(f32)
    log_a = jax.nn.log_sigmoid(a_logit).reshape(B, L, GLA_HEADS, GLA_DK) / GLA_TAU
    o, s_fin = gla_chunked(q, k, v, log_a, s0.astype(f32), gla_chunk)
    o = o * lax.rsqrt(jnp.mean(o * o, axis=-1, keepdims=True) + EPS)
    o = o.reshape(B, L, GLA_WIDTH) * p['gla_norm'].astype(f32) * jax.nn.silu(gg.astype(f32))
    mq = partial_rope(bq.reshape(B, L, MOBA_HEADS, MOBA_HD), pos)
    mk = partial_rope(bk.reshape(B, L, MOBA_HEADS, MOBA_HD), pos)
    mv = bv.reshape(B, L, MOBA_HEADS, MOBA_HD)
    mo = attend(mq, mk, mv, pos).reshape(B, L, MOBA_WIDTH)
    mix = jnp.concatenate([o.astype(x.dtype), mo.astype(x.dtype)], axis=-1)
    x = x + jnp.einsum('ble,ed->bld', mix, p['w_out']).astype(x.dtype)
    x = x + mem_attend(rmsnorm(x, p['norm_mem_q']), mem_k, mem_v, p['w_mq'], p['w_mo']).astype(x.dtype)
    x = x + hier_moe(rmsnorm(x, p['norm_ffn']), p['w_grp'], p['b_grp'], p['w_router'], p['b_router'],
                     p['w_e_gate'], p['w_e_up'], p['w_e_down']).astype(x.dtype)
    return x, mk, mv, s_fin


def setup_inputs(seed: int = 0) -> dict:
    key = jax.random.key(seed)
    ks = jax.random.split(key, 32)
    f32 = jnp.float32
    n_pages = PAST_LEN // PAGE_SIZE
    n_pool = (DEC_BATCH * n_pages * 5) // 4

    def nrm(k, shape, scale=1.0):
        return jax.random.normal(k, shape, f32) * scale

    page_table = jax.random.permutation(ks[9], n_pool)[:DEC_BATCH * n_pages].reshape(DEC_BATCH, n_pages).astype(jnp.int32)
    return {
        'x_prompt': nrm(ks[0], (BATCH, SEQ, D_MODEL)),
        'x_sample': nrm(ks[1], (DEC_BATCH, DEC_SEQ, D_MODEL)),
        'mem_prompt': nrm(ks[2], (BATCH, MEM_TOKENS, D_MODEL)),
        'cache_moba_k': nrm(ks[3], (DEPTH, n_pool, PAGE_SIZE, MOBA_HEADS, MOBA_HD)),
        'cache_moba_v': nrm(ks[4], (DEPTH, n_pool, PAGE_SIZE, MOBA_HEADS, MOBA_HD)),
        'state_gla': nrm(ks[5], (DEPTH, DEC_BATCH, GLA_HEADS, GLA_DK, GLA_DV), 0.5),
        'cache_mem_k': nrm(ks[6], (DEPTH, DEC_BATCH, MEM_TOKENS, MEM_HEADS, MEM_HD)),
        'cache_mem_v': nrm(ks[7], (DEPTH, DEC_BATCH, MEM_TOKENS, MEM_HEADS, MEM_HD)),
        'page_table': page_table,
        'norm_mix': 1.0 + nrm(ks[10], (DEPTH, D_MODEL), 0.01),
        'w_in': nrm(ks[11], (DEPTH, D_MODEL, D_IN), D_MODEL ** -0.5),
        'w_a2': nrm(ks[12], (DEPTH, GLA_RANK, GLA_HEADS * GLA_DK), GLA_RANK ** -0.5),
        'b_a': nrm(ks[13], (DEPTH, GLA_HEADS * GLA_DK), 0.1),
        'gla_norm': 1.0 + nrm(ks[14], (DEPTH, GLA_WIDTH), 0.01),
        'w_out': nrm(ks[15], (DEPTH, D_MIX, D_MODEL), D_MIX ** -0.5),
        'norm_mem_q': 1.0 + nrm(ks[16], (DEPTH, D_MODEL), 0.01),
        'norm_mem_kv': 1.0 + nrm(ks[17], (DEPTH, D_MODEL), 0.01),
        'w_mq': nrm(ks[18], (DEPTH, D_MODEL, MEM_WIDTH), D_MODEL ** -0.5),
        'w_mk': nrm(ks[19], (DEPTH, D_MODEL, MEM_WIDTH), D_MODEL ** -0.5),
        'w_mv': nrm(ks[20], (DEPTH, D_MODEL, MEM_WIDTH), D_MODEL ** -0.5),
        'w_mo': nrm(ks[21], (DEPTH, MEM_WIDTH, D_MODEL), MEM_WIDTH ** -0.5),
        'norm_ffn': 1.0 + nrm(ks[22], (DEPTH, D_MODEL), 0.01),
        'w_grp': nrm(ks[23], (DEPTH, D_MODEL, N_GROUPS), D_MODEL ** -0.5),
        'b_grp': nrm(ks[24], (DEPTH, N_GROUPS), 0.01),
        'w_router': nrm(ks[25], (DEPTH, D_MODEL, N_EXPERTS), D_MODEL ** -0.5),
        'b_router': nrm(ks[26], (DEPTH, N_EXPERTS), 0.01),
        'w_e_gate': nrm(ks[27], (DEPTH, N_EXPERTS, D_MODEL, D_EXPERT), D_MODEL ** -0.5),
        'w_e_up': nrm(ks[28], (DEPTH, N_EXPERTS, D_MODEL, D_EXPERT), D_MODEL ** -0.5),
        'w_e_down': nrm(ks[29], (DEPTH, N_EXPERTS, D_EXPERT, D_MODEL), D_EXPERT ** -0.5),
        'norm_final': 1.0 + nrm(ks[30], (D_MODEL,), 0.01),
    }


def reference(x_prompt, x_sample, mem_prompt, cache_moba_k, cache_moba_v, state_gla, cache_mem_k, cache_mem_v,
              page_table, norm_mix, w_in, w_a2, b_a, gla_norm, w_out, norm_mem_q, norm_mem_kv, w_mq, w_mk, w_mv,
              w_mo, norm_ffn, w_grp, b_grp, w_router, b_router, w_e_gate, w_e_up, w_e_down, norm_final):
    past = page_table.shape[1] * cache_moba_k.shape[2]
    Lp = x_prompt.shape[1]
    Ls = x_sample.shape[1]
    pos_p = jnp.arange(Lp, dtype=jnp.int32)
    pos_s = past + jnp.arange(Ls, dtype=jnp.int32)
    s0_p = jnp.zeros((x_prompt.shape[0], GLA_HEADS, GLA_DK, GLA_DV), jnp.float32)
    chunk_p = math.gcd(GLA_CHUNK, Lp)
    chunk_s = math.gcd(GLA_CHUNK, Ls)
    xp, xs = x_prompt, x_sample
    kp_l, vp_l, sp_l, mkp_l, mvp_l, ks_l, vs_l, ss_l = [], [], [], [], [], [], [], []
    for l in range(DEPTH):
        p = dict(norm_mix=norm_mix[l], w_in=w_in[l], w_a2=w_a2[l], b_a=b_a[l], gla_norm=gla_norm[l],
                 w_out=w_out[l], norm_mem_q=norm_mem_q[l], w_mq=w_mq[l], w_mo=w_mo[l], norm_ffn=norm_ffn[l],
                 w_grp=w_grp[l], b_grp=b_grp[l], w_router=w_router[l], b_router=b_router[l],
                 w_e_gate=w_e_gate[l], w_e_up=w_e_up[l], w_e_down=w_e_down[l])
        mk_p, mv_p = mem_kv(mem_prompt, norm_mem_kv[l], w_mk[l], w_mv[l])
        xp, kp, vp, sp = layer_forward(xp, pos_p, s0_p, mk_p, mv_p, prompt_moba, p, chunk_p)
        ck = cache_moba_k[l]
        cv = cache_moba_v[l]

        def attend_s(q, k, v, pos, ck=ck, cv=cv):
            return sample_moba(q, k, v, pos, ck, cv, page_table)

        xs, ks, vs, ss = layer_forward(xs, pos_s, state_gla[l], cache_mem_k[l], cache_mem_v[l], attend_s, p, chunk_s)
        kp_l.append(kp)
        vp_l.append(vp)
        sp_l.append(sp)
        mkp_l.append(mk_p)
        mvp_l.append(mv_p)
        ks_l.append(ks)
        vs_l.append(vs)
        ss_l.append(ss)
    y_prompt = rmsnorm(xp, norm_final)
    y_sample = rmsnorm(xs, norm_final)
    return (y_prompt, y_sample, jnp.stack(kp_l), jnp.stack(vp_l), jnp.stack(sp_l), jnp.stack(mkp_l),
            jnp.stack(mvp_l), jnp.stack(ks_l), jnp.stack(vs_l), jnp.stack(ss_l))
```

```python
import functools
import math

import numpy as np
import jax
import jax.numpy as jnp
from jax import lax
from jax.experimental import pallas as pl
from jax.experimental.pallas import tpu as pltpu

F32 = jnp.float32
BF16 = jnp.bfloat16

D_MODEL = 1024
GLA_HEADS = 4
GLA_DK = 64
GLA_DV = 128
GLA_KW = GLA_HEADS * GLA_DK
GLA_WIDTH = GLA_HEADS * GLA_DV
GLA_RANK = 16
GLA_TAU = 16.0
GLA_CHUNK = 64
MOBA_HEADS = 8
MOBA_HD = 64
MOBA_WIDTH = MOBA_HEADS * MOBA_HD
MOBA_BLOCK = 256
MOBA_TOPK = 3
ROPE_THETA = 500000.0
ROT_DIM = MOBA_HD // 4
MEM_HEADS = 4
MEM_HD = 128
MEM_WIDTH = MEM_HEADS * MEM_HD
N_GROUPS = 4
EXPERTS_PER_GROUP = 4
N_EXPERTS = N_GROUPS * EXPERTS_PER_GROUP
D_EXPERT = D_MODEL // 2
EPS = 1e-6

LANES = 128
NEG = -1e30
VMEM_LIMIT = 56 * 1024 * 1024

NN = ((1,), (0,))
NT = ((1,), (1,))
TN = ((0,), (0,))

C_GQ, C_GK, C_GV, C_GG, C_BQ, C_BK, C_BV, C_GA, C_END = 0, 256, 512, 1024, 1536, 2048, 2560, 3072, 3200


def _dot(a, b, dims=NN):
    return lax.dot_general(a, b, (dims, ((), ())), preferred_element_type=F32)


def _hi_lo(x):
    hi = x.astype(BF16)
    lo = (x - hi.astype(F32)).astype(BF16)
    return hi, lo


def _dot_split(a, b, dims=NN):
    ah, al = _hi_lo(a)
    bh, bl = _hi_lo(b)
    return _dot(ah, bh, dims) + (_dot(ah, bl, dims) + _dot(al, bh, dims))


def _dot_exact_lhs(a_bf16, x):
    h1 = x.astype(BF16)
    r1 = x - h1.astype(F32)
    h2 = r1.astype(BF16)
    h3 = (r1 - h2.astype(F32)).astype(BF16)
    return _dot(a_bf16, h1) + (_dot(a_bf16, h2) + _dot(a_bf16, h3))


def _rms(x, g):
    return x * lax.rsqrt(jnp.mean(x * x, axis=-1, keepdims=True) + EPS) * g


def _params(*sem):
    return pltpu.CompilerParams(dimension_semantics=sem, vmem_limit_bytes=VMEM_LIMIT)


def _const_spec(shape):
    nd = len(shape)
    return pl.BlockSpec(shape, lambda *_: (0,) * nd)


def _row_tile(n, cap):
    t = min(n, cap)
    assert n % t == 0, (n, t)
    return t


def _inproj_kernel(x_ref, g_ref, w_ref, wah_ref, wal_ref, ba_ref, cos_ref, sa_ref, sb_ref,
                   gq_ref, gk_ref, gv_ref, gg_ref, la_ref, mq_ref, mk_ref, mv_ref):
    h = _rms(x_ref[...], g_ref[...]).astype(BF16)

    def proj(lo, hi):
        return _dot(h, w_ref[:, lo:hi])

    def rope(xf):
        cos, sa, sb = cos_ref[...], sa_ref[...], sb_ref[...]
        half = ROT_DIM // 2
        outs = []
        for c in range(MOBA_WIDTH // LANES):
            xc = xf[:, c * LANES:(c + 1) * LANES]
            outs.append(xc * cos + pltpu.roll(xc, LANES - half, 1) * sa + pltpu.roll(xc, half, 1) * sb)
        return jnp.concatenate(outs, axis=1)

    gq_ref[...] = proj(C_GQ, C_GK) * (GLA_DK ** -0.5)
    gk_ref[...] = proj(C_GK, C_GV)
    gv_ref[...] = proj(C_GV, C_GG)
    gg_ref[...] = proj(C_GG, C_BQ)
    ga = proj(C_GA, C_END)
    gah, gal = _hi_lo(ga)
    a = _dot(gah, wah_ref[...]) + (_dot(gah, wal_ref[...]) + _dot(gal, wah_ref[...])) + ba_ref[...]
    la_ref[...] = (jnp.minimum(a, 0.0) - jnp.log1p(jnp.exp(-jnp.abs(a)))) * (1.0 / GLA_TAU)
    mq_ref[...] = rope(proj(C_BQ, C_BK)) * (MOBA_HD ** -0.5)
    mk_ref[...] = rope(proj(C_BK, C_BV))
    mv_ref[...] = proj(C_BV, C_GA)


def _inproj(x, g, w_bf, wa_h, wa_l, ba, cos_t, sa_t, sb_t):
    n = x.shape[0]
    tm = _row_tile(n, 512)
    period = cos_t.shape[0]
    assert period % tm == 0
    nper = period // tm
    row = lambda w: pl.BlockSpec((tm, w), lambda i: (i, 0))
    tab = pl.BlockSpec((tm, LANES), lambda i: (i % nper, 0))
    widths = (GLA_KW, GLA_KW, GLA_WIDTH, GLA_WIDTH, GLA_KW, MOBA_WIDTH, MOBA_WIDTH, MOBA_WIDTH)
    return pl.pallas_call(
        _inproj_kernel,
        grid=(n // tm,),
        in_specs=[row(D_MODEL), _const_spec((1, D_MODEL)), _const_spec(w_bf.shape), _const_spec(wa_h.shape),
                  _const_spec(wa_l.shape), _const_spec((1, GLA_KW)), tab, tab, tab],
        out_specs=[row(w) for w in widths],
        out_shape=[jax.ShapeDtypeStruct((n, w), F32) for w in widths],
        compiler_params=_params("parallel"),
        name="inproj",
    )(x, g, w_bf, wa_h, wa_l, ba, cos_t, sa_t, sb_t)


def _gla_kernel(q_ref, k_ref, v_ref, gg_ref, la_ref, s0_ref, gn_ref, o_ref, sfin_ref, st_sc, *, chunk, nchunks):
    t = pl.program_id(1)

    @pl.when(t == 0)
    def _():
        st_sc[...] = s0_ref[0].T

    lane = lax.broadcasted_iota(jnp.int32, (chunk, GLA_KW), 1)
    tri = lax.broadcasted_iota(jnp.int32, (chunk, chunk), 0) >= lax.broadcasted_iota(jnp.int32, (chunk, chunk), 1)
    tri_bf = tri.astype(F32).astype(BF16)
    gn = gn_ref[...]
    for c in range(nchunks):
        sl = slice(c * chunk, (c + 1) * chunk)
        b = _dot_exact_lhs(tri_bf, la_ref[0, sl, :])
        b_last = b[chunk - 1:chunk, :]
        q_dec = q_ref[0, sl, :] * jnp.exp(b)
        k = k_ref[0, sl, :]
        k_inv = (k * jnp.exp(-b)).astype(BF16)
        k_rem = k * jnp.exp(b_last - b)
        st = st_sc[...]
        st_bf = st.astype(BF16)
        upd = st * jnp.exp(b_last)
        for h in range(GLA_HEADS):
            hm = (lane // GLA_DK) == h
            qm = jnp.where(hm, q_dec, 0.0).astype(BF16)
            attn = jnp.where(tri, _dot(qm, k_inv, NT), 0.0).astype(BF16)
            hs = slice(h * GLA_DV, (h + 1) * GLA_DV)
            vh = v_ref[0, sl, hs].astype(BF16)
            oh = _dot(attn, vh) + _dot(qm, st_bf, NT)
            oh = oh * lax.rsqrt(jnp.mean(oh * oh, axis=-1, keepdims=True) + EPS)
            gg = gg_ref[0, sl, hs]
            o_ref[0, sl, hs] = (oh * gn[:, hs] * (gg * jax.nn.sigmoid(gg))).astype(o_ref.dtype)
            upd = upd + _dot(vh, jnp.where(hm, k_rem, 0.0).astype(BF16), TN)
        st_sc[...] = upd

    @pl.when(t == pl.num_programs(1) - 1)
    def _():
        sfin_ref[0] = st_sc[...].T


def _gla(gq, gk, gv, gg, la, s0, gn, chunk):
    bsz, L, _ = gq.shape
    nchunks = max(1, min(8, L // chunk))
    tb = chunk * nchunks
    assert L % tb == 0
    seq = lambda w: pl.BlockSpec((1, tb, w), lambda b, t: (b, t, 0))
    st = pl.BlockSpec((1, GLA_KW, GLA_DV), lambda b, t: (b, 0, 0))
    return pl.pallas_call(
        functools.partial(_gla_kernel, chunk=chunk, nchunks=nchunks),
        grid=(bsz, L // tb),
        in_specs=[seq(GLA_KW), seq(GLA_KW), seq(GLA_WIDTH), seq(GLA_WIDTH), seq(GLA_KW), st,
                  _const_spec((1, GLA_WIDTH))],
        out_specs=[seq(GLA_WIDTH), st],
        out_shape=[jax.ShapeDtypeStruct((bsz, L, GLA_WIDTH), BF16),
                   jax.ShapeDtypeStruct((bsz, GLA_KW, GLA_DV), F32)],
        scratch_shapes=[pltpu.VMEM((GLA_DV, GLA_KW), F32)],
        compiler_params=_params("parallel", "arbitrary"),
        name="gla",
    )(gq, gk, gv, gg, la, s0, gn)


def _blockmean_kernel(k_ref, o_ref, *, nbb):
    k = k_ref[0]
    o_ref[0] = jnp.sum(k.reshape(nbb, MOBA_BLOCK, MOBA_WIDTH), axis=1) * (1.0 / MOBA_BLOCK)


def _blockmeans(mk):
    bsz, L, _ = mk.shape
    nb = L // MOBA_BLOCK
    nbb = 8 if nb % 8 == 0 else nb
    return pl.pallas_call(
        functools.partial(_blockmean_kernel, nbb=nbb),
        grid=(bsz, nb // nbb),
        in_specs=[pl.BlockSpec((1, nbb * MOBA_BLOCK, MOBA_WIDTH), lambda b, i: (b, i, 0))],
        out_specs=pl.BlockSpec((1, nbb, MOBA_WIDTH), lambda b, i: (b, i, 0)),
        out_shape=jax.ShapeDtypeStruct((bsz, nb, MOBA_WIDTH), F32),
        compiler_params=_params("parallel", "parallel"),
        name="moba_blockmeans",
    )(mk)


def _top3_bias(gate, blk, own, keep_own):
    g = jnp.where(blk < own, gate, -jnp.inf)
    bias = jnp.where(blk == own, 0.0, NEG) if keep_own else jnp.full(gate.shape, NEG, F32)
    for _ in range(MOBA_TOPK):
        m = jnp.max(g, axis=1, keepdims=True)
        idx = jnp.min(jnp.where(g == m, blk, float(LANES)), axis=1, keepdims=True)
        pick = (blk == idx) & (m > -jnp.inf)
        bias = jnp.where(pick, 0.0, bias)
        g = jnp.where(pick, -jnp.inf, g)
    return bias


def _moba_prep_kernel(q_ref, k_ref, v_ref, mean_ref, qa_ref, ka_ref, va_ref):
    i = pl.program_id(1)
    q = q_ref[0]
    k = k_ref[0]
    v = v_ref[0]
    mean = mean_ref[0]
    mlane = lax.broadcasted_iota(jnp.int32, mean.shape, 1)
    lane_i = lax.broadcasted_iota(jnp.int32, (MOBA_BLOCK, LANES), 1)
    lane_f = lane_i.astype(F32)
    own = i.astype(F32)
    first = lane_i < MOBA_HD
    onehot = jnp.where(lane_i == i + MOBA_HD, 1.0, 0.0)
    qh, ql = _hi_lo(q)
    for h in range(MOBA_HEADS):
        mh = jnp.where((mlane // MOBA_HD) == h, mean, 0.0)
        mhh, mhl = _hi_lo(mh)
        gate = _dot(qh, mhh, NT) + (_dot(qh, mhl, NT) + _dot(ql, mhh, NT))
        bias = pltpu.roll(_top3_bias(gate, lane_f, own, True), MOBA_HD, 1)
        c = (h // 2) * LANES
        if h % 2 == 0:
            q128, k128, v128 = q[:, c:c + LANES], k[:, c:c + LANES], v[:, c:c + LANES]
        else:
            q128 = pltpu.roll(q[:, c:c + LANES], MOBA_HD, 1)
            k128 = pltpu.roll(k[:, c:c + LANES], MOBA_HD, 1)
            v128 = pltpu.roll(v[:, c:c + LANES], MOBA_HD, 1)
        qa_ref[0, h] = jnp.where(first, q128, bias).astype(BF16)
        ka_ref[0, h] = jnp.where(first, k128, onehot).astype(BF16)
        va_ref[0, h] = jnp.where(first, v128, 1.0).astype(BF16)


def _moba_prep(mq, mk, mv, means_pad):
    bsz, L, _ = mq.shape
    nb = L // MOBA_BLOCK
    seq = pl.BlockSpec((1, MOBA_BLOCK, MOBA_WIDTH), lambda b, i: (b, i, 0))
    aug = pl.BlockSpec((1, MOBA_HEADS, MOBA_BLOCK, LANES), lambda b, i: (b, 0, i, 0))
    shp = jax.ShapeDtypeStruct((bsz, MOBA_HEADS, L, LANES), BF16)
    return pl.pallas_call(
        _moba_prep_kernel,
        grid=(bsz, nb),
        in_specs=[seq, seq, seq, pl.BlockSpec((1, LANES, MOBA_WIDTH), lambda b, i: (b, 0, 0))],
        out_specs=[aug, aug, aug],
        out_shape=[shp, shp, shp],
        compiler_params=_params("parallel", "parallel"),
        name="moba_prep",
    )(mq, mk, mv, means_pad)


def _moba_attn_kernel(q_ref, k_ref, v_ref, o_ref):
    i = pl.program_id(2)
    row = lax.broadcasted_iota(jnp.int32, (MOBA_BLOCK, MOBA_BLOCK), 0)
    col = lax.broadcasted_iota(jnp.int32, (MOBA_BLOCK, MOBA_BLOCK), 1)
    causal = col <= row
    qs = [q_ref[0, hh] for hh in range(2)]

    def kv(hh, j):
        off = pl.multiple_of(j * MOBA_BLOCK, MOBA_BLOCK)
        return k_ref[0, hh, pl.ds(off, MOBA_BLOCK), :], v_ref[0, hh, pl.ds(off, MOBA_BLOCK), :]

    init = []
    for hh in range(2):
        k, v = kv(hh, i)
        s = jnp.where(causal, _dot(qs[hh], k, NT), NEG)
        m = jnp.max(s, axis=1, keepdims=True)
        p = jnp.exp(s - m)
        init += [m, _dot(p.astype(BF16), v)]

    def body(j, st):
        new = []
        for hh in range(2):
            m, acc = st[2 * hh], st[2 * hh + 1]
            k, v = kv(hh, j)
            s = _dot(qs[hh], k, NT)
            m_new = jnp.maximum(m, jnp.max(s, axis=1, keepdims=True))
            p = jnp.exp(s - m_new)
            new += [m_new, jnp.exp(m - m_new) * acc + _dot(p.astype(BF16), v)]
        return tuple(new)

    st = lax.fori_loop(0, i, body, tuple(init))
    outs = [st[2 * hh + 1] / pltpu.roll(st[2 * hh + 1], MOBA_HD, 1) for hh in range(2)]
    first = lax.broadcasted_iota(jnp.int32, (MOBA_BLOCK, LANES), 1) < MOBA_HD
    o_ref[0] = jnp.where(first, outs[0], pltpu.roll(outs[1], MOBA_HD, 1)).astype(o_ref.dtype)


def _moba_attn(qa, ka, va):
    bsz, _, L, _ = qa.shape
    nb = L // MOBA_BLOCK
    npair = MOBA_HEADS // 2
    return pl.pallas_call(
        _moba_attn_kernel,
        grid=(bsz, npair, nb),
        in_specs=[pl.BlockSpec((1, 2, MOBA_BLOCK, LANES), lambda b, p, i: (b, p, i, 0)),
                  pl.BlockSpec((1, 2, L, LANES), lambda b, p, i: (b, p, 0, 0)),
                  pl.BlockSpec((1, 2, L, LANES), lambda b, p, i: (b, p, 0, 0))],
        out_specs=pl.BlockSpec((1, MOBA_BLOCK, LANES), lambda b, p, i: (b, i, p)),
        out_shape=jax.ShapeDtypeStruct((bsz, L, MOBA_WIDTH), BF16),
        compiler_params=_params("parallel", "parallel", "arbitrary"),
        name="moba_attn",
    )(qa, ka, va)


def _smoba_kernel(pt_ref, *refs, pages_per_step, page_rows, nblk, ls):
    pps = pages_per_step
    kp = refs[:pps]
    vp = refs[pps:2 * pps]
    q_ref, kn_ref, vn_ref, own_ref, o_ref, opart_sc, m_sc, l_sc, ksum_sc = refs[2 * pps:]
    del pt_ref
    step = pl.program_id(1)
    rows = MOBA_HEADS * ls
    ppb = MOBA_BLOCK // page_rows
    bps = pps // ppb

    rlane = lax.broadcasted_iota(jnp.int32, (rows, MOBA_WIDTH), 1)
    rrow = lax.broadcasted_iota(jnp.int32, (rows, MOBA_WIDTH), 0)
    diag = (rlane // MOBA_HD) == (rrow // ls)
    q = q_ref[0]
    qbd = jnp.where(diag, jnp.concatenate([q] * MOBA_HEADS, axis=0), 0.0)
    qbd_bf = qbd.astype(BF16)
    lane128 = lax.broadcasted_iota(jnp.int32, (rows, LANES), 1)

    @pl.when(step == 0)
    def _():
        ksum_sc[...] = jnp.zeros_like(ksum_sc)
        m_sc[...] = jnp.full(m_sc.shape, NEG, F32)
        l_sc[...] = jnp.zeros_like(l_sc)

    for bl in range(bps):
        jb = step * bps + bl
        kb = jnp.concatenate([kp[bl * ppb + r][0] for r in range(ppb)], axis=0)
        vb = jnp.concatenate([vp[bl * ppb + r][0] for r in range(ppb)], axis=0)
        s = _dot(qbd_bf, kb.astype(BF16), NT)
        m = jnp.max(s, axis=1, keepdims=True)
        p = jnp.exp(s - m)
        opart_sc[jb] = _dot(p.astype(BF16), vb.astype(BF16))
        m_sc[...] = jnp.where(lane128 == jb, m, m_sc[...])
        l_sc[...] = jnp.where(lane128 == jb, jnp.sum(p, axis=1, keepdims=True), l_sc[...])
        ksum_sc[pl.ds(jb, 1), :] = jnp.sum(kb, axis=0, keepdims=True)

    @pl.when(step == pl.num_programs(1) - 1)
    def _():
        means = ksum_sc[...] * (1.0 / MOBA_BLOCK)
        gate = _dot_split(qbd, means, NT)
        blk = lane128.astype(F32)
        own = own_ref[...]
        sel = _top3_bias(gate, blk, own, False) == 0.0
        s_own = _dot(qbd_bf, kn_ref[0].astype(BF16), NT)
        qi = lax.broadcasted_iota(jnp.int32, (rows, ls), 0) % ls
        ki = lax.broadcasted_iota(jnp.int32, (rows, ls), 1)
        s_own = jnp.where(ki <= qi, s_own, NEG)
        m_own = jnp.max(s_own, axis=1, keepdims=True)
        m_all = m_sc[...]
        m_tot = jnp.maximum(m_own, jnp.max(jnp.where(sel, m_all, NEG), axis=1, keepdims=True))
        p_own = jnp.exp(s_own - m_tot)
        w = jnp.where(sel, jnp.exp(m_all - m_tot), 0.0)
        den = jnp.sum(p_own, axis=1, keepdims=True) + jnp.sum(w * l_sc[...], axis=1, keepdims=True)
        num = _dot(p_own.astype(BF16), vn_ref[0].astype(BF16))
        for j in range(nblk):
            num = num + w[:, j:j + 1] * opart_sc[j]
        full = jnp.where(diag, num / den, 0.0)
        out = full[0:ls]
        for h in range(1, MOBA_HEADS):
            out = out + full[h * ls:(h + 1) * ls]
        o_ref[0] = out.astype(o_ref.dtype)


def _smoba(page_table, cache_k, cache_v, mq, mk, mv, own_rows):
    bsz, ls, _ = mq.shape
    n_pages = page_table.shape[1]
    page_rows = cache_k.shape[1]
    ppb = MOBA_BLOCK // page_rows
    nblk = n_pages // ppb
    pps = math.gcd(n_pages, 8)
    assert pps % ppb == 0 and nblk <= LANES
    rows = MOBA_HEADS * ls

    def page_spec(r):
        return pl.BlockSpec((1, page_rows, MOBA_WIDTH), lambda b, s, pt: (pt[b, s * pps + r], 0, 0))

    seq = pl.BlockSpec((1, ls, MOBA_WIDTH), lambda b, s, pt: (b, 0, 0))
    grid_spec = pltpu.PrefetchScalarGridSpec(
        num_scalar_prefetch=1,
        grid=(bsz, n_pages // pps),
        in_specs=[page_spec(r) for r in range(pps)] + [page_spec(r) for r in range(pps)]
        + [seq, seq, seq, pl.BlockSpec((rows, 1), lambda b, s, pt: (0, 0))],
        out_specs=seq,
        scratch_shapes=[pltpu.VMEM((nblk, rows, MOBA_WIDTH), F32), pltpu.VMEM((rows, LANES), F32),
                        pltpu.VMEM((rows, LANES), F32), pltpu.VMEM((LANES, MOBA_WIDTH), F32)],
    )
    return pl.pallas_call(
        functools.partial(_smoba_kernel, pages_per_step=pps, page_rows=page_rows, nblk=nblk, ls=ls),
        grid_spec=grid_spec,
        out_shape=jax.ShapeDtypeStruct((bsz, ls, MOBA_WIDTH), BF16),
        compiler_params=_params("parallel", "arbitrary"),
        name="moba_paged",
    )(page_table, *([cache_k] * pps), *([cache_v] * pps), mq, mk, mv, own_rows)


def _outproj_kernel(x_ref, og_ref, mo_ref, wo_ref, g_ref, wq_ref, x1_ref, qm_ref):
    x1 = x_ref[...] + (_dot(og_ref[...], wo_ref[0:GLA_WIDTH, :]) + _dot(mo_ref[...], wo_ref[GLA_WIDTH:, :]))
    x1_ref[...] = x1
    h = _rms(x1, g_ref[...]).astype(BF16)
    qm_ref[...] = _dot(h, wq_ref[...]).astype(BF16)


def _outproj(x, og, mo, wo_bf, g, wq_bf):
    n = x.shape[0]
    tm = _row_tile(n, 512)
    row = lambda w: pl.BlockSpec((tm, w), lambda i: (i, 0))
    return pl.pallas_call(
        _outproj_kernel,
        grid=(n // tm,),
        in_specs=[row(D_MODEL), row(GLA_WIDTH), row(MOBA_WIDTH), _const_spec(wo_bf.shape),
                  _const_spec((1, D_MODEL)), _const_spec(wq_bf.shape)],
        out_specs=[row(D_MODEL), row(MEM_WIDTH)],
        out_shape=[jax.ShapeDtypeStruct((n, D_MODEL), F32), jax.ShapeDtypeStruct((n, MEM_WIDTH), BF16)],
        compiler_params=_params("parallel"),
        name="outproj",
    )(x, og, mo, wo_bf, g, wq_bf)


def _memkv_kernel(m_ref, g_ref, wk_ref, wv_ref, k_ref, v_ref):
    h = _rms(m_ref[0], g_ref[...]).astype(BF16)
    k_ref[0] = _dot(h, wk_ref[...])
    v_ref[0] = _dot(h, wv_ref[...])


def _memkv(mem, g, wk_bf, wv_bf):
    bsz, m, _ = mem.shape
    out = pl.BlockSpec((1, m, MEM_WIDTH), lambda b: (b, 0, 0))
    shp = jax.ShapeDtypeStruct((bsz, m, MEM_WIDTH), F32)
    return pl.pallas_call(
        _memkv_kernel,
        grid=(bsz,),
        in_specs=[pl.BlockSpec((1, m, D_MODEL), lambda b: (b, 0, 0)), _const_spec((1, D_MODEL)),
                  _const_spec(wk_bf.shape), _const_spec(wv_bf.shape)],
        out_specs=[out, out],
        out_shape=[shp, shp],
        compiler_params=_params("parallel"),
        name="mem_kv",
    )(mem, g, wk_bf, wv_bf)


def _memattn_kernel(q_ref, k_ref, v_ref, o_ref):
    q = q_ref[0]
    for h in range(MEM_HEADS):
        hs = slice(h * MEM_HD, (h + 1) * MEM_HD)
        s = _dot(q[:, hs], k_ref[0, :, hs].astype(BF16), NT) * (MEM_HD ** -0.5)
        p = jnp.exp(s - jnp.max(s, axis=1, keepdims=True))
        o = _dot(p.astype(BF16), v_ref[0, :, hs].astype(BF16)) / jnp.sum(p, axis=1, keepdims=True)
        o_ref[0, :, hs] = o.astype(o_ref.dtype)


def _memattn(qm, mem_k, mem_v):
    bsz, L, _ = qm.shape
    m = mem_k.shape[1]
    tl = _row_tile(L, 512)
    seq = pl.BlockSpec((1, tl, MEM_WIDTH), lambda b, t: (b, t, 0))
    kv = pl.BlockSpec((1, m, MEM_WIDTH), lambda b, t: (b, 0, 0))
    return pl.pallas_call(
        _memattn_kernel,
        grid=(bsz, L // tl),
        in_specs=[seq, kv, kv],
        out_specs=seq,
        out_shape=jax.ShapeDtypeStruct((bsz, L, MEM_WIDTH), BF16),
        compiler_params=_params("parallel", "parallel"),
        name="mem_attn",
    )(qm, mem_k, mem_v)


def _memout_router_kernel(x1_ref, om_ref, wmo_ref, g_ref, wrh_ref, wrl_ref, br_ref, x2_ref, h_ref, gates_ref):
    x2 = x1_ref[...] + _dot(om_ref[...], wmo_ref[...])
    x2_ref[...] = x2
    h = _rms(x2, g_ref[...])
    h_ref[...] = h.astype(BF16)
    hh, hl = _hi_lo(h)
    logits = _dot(hh, wrh_ref[...]) + (_dot(hh, wrl_ref[...]) + _dot(hl, wrh_ref[...])) + br_ref[...]
    lane = lax.broadcasted_iota(jnp.int32, logits.shape, 1)
    lane_f = lane.astype(F32)
    big = float(LANES)
    is_g = lane < N_GROUPS
    gl = jnp.where(is_g, logits, -jnp.inf)
    gmax = jnp.max(gl, axis=1, keepdims=True)
    grp = jnp.min(jnp.where(gl == gmax, lane_f, big), axis=1, keepdims=True)
    p_grp = 1.0 / jnp.sum(jnp.where(is_g, jnp.exp(gl - gmax), 0.0), axis=1, keepdims=True)
    lo = N_GROUPS + EXPERTS_PER_GROUP * grp
    in_grp = (lane_f >= lo) & (lane_f < lo + EXPERTS_PER_GROUP)
    el = jnp.where(in_grp, logits, -jnp.inf)
    e1 = jnp.max(el, axis=1, keepdims=True)
    i1 = jnp.min(jnp.where(el == e1, lane_f, big), axis=1, keepdims=True)
    el2 = jnp.where(lane_f == i1, -jnp.inf, el)
    e2 = jnp.max(el2, axis=1, keepdims=True)
    i2 = jnp.min(jnp.where(el2 == e2, lane_f, big), axis=1, keepdims=True)
    r = jnp.exp(e2 - e1)
    p1 = 1.0 / (1.0 + r)
    gates_ref[...] = jnp.where(lane_f == i1, p_grp * p1, 0.0) + jnp.where(lane_f == i2, p_grp * (r * p1), 0.0)


def _memout_router(x1, om, wmo_bf, g, wr_h, wr_l, br):
    n = x1.shape[0]
    tm = _row_tile(n, 512)
    row = lambda w: pl.BlockSpec((tm, w), lambda i: (i, 0))
    return pl.pallas_call(
        _memout_router_kernel,
        grid=(n // tm,),
        in_specs=[row(D_MODEL), row(MEM_WIDTH), _const_spec(wmo_bf.shape), _const_spec((1, D_MODEL)),
                  _const_spec(wr_h.shape), _const_spec(wr_l.shape), _const_spec((1, LANES))],
        out_specs=[row(D_MODEL), row(D_MODEL), row(LANES)],
        out_shape=[jax.ShapeDtypeStruct((n, D_MODEL), F32), jax.ShapeDtypeStruct((n, D_MODEL), BF16),
                   jax.ShapeDtypeStruct((n, LANES), F32)],
        compiler_params=_params("parallel"),
        name="memout_router",
    )(x1, om, wmo_bf, g, wr_h, wr_l, br)


def _moe_kernel(h_ref, gates_ref, x2_ref, wg_ref, wu_ref, wd_ref, gf_ref, y_ref, acc_sc):
    e = pl.program_id(1)

    @pl.when(e == 0)
    def _():
        acc_sc[...] = jnp.zeros_like(acc_sc)

    h = h_ref[...]
    a = _dot(h, wg_ref[0])
    hid = (a * jax.nn.sigmoid(a)) * _dot(h, wu_ref[0])
    gates = gates_ref[...]
    lane = lax.broadcasted_iota(jnp.int32, gates.shape, 1)
    ge = jnp.sum(jnp.where(lane == e + N_GROUPS, gates, 0.0), axis=1, keepdims=True)
    acc_sc[...] += ge * _dot(hid.astype(BF16), wd_ref[0])

    @pl.when(e == pl.num_programs(1) - 1)
    def _():
        y_ref[...] = _rms(x2_ref[...] + acc_sc[...], gf_ref[...])


def _moe(h, gates, x2, wg_bf, wu_bf, wd_bf, gf):
    n = h.shape[0]
    tm = _row_tile(n, 512)
    row = lambda w: pl.BlockSpec((tm, w), lambda i, e: (i, 0))
    return pl.pallas_call(
        _moe_kernel,
        grid=(n // tm, N_EXPERTS),
        in_specs=[row(D_MODEL), row(LANES), row(D_MODEL),
                  pl.BlockSpec((1, D_MODEL, D_EXPERT), lambda i, e: (e, 0, 0)),
                  pl.BlockSpec((1, D_MODEL, D_EXPERT), lambda i, e: (e, 0, 0)),
                  pl.BlockSpec((1, D_EXPERT, D_MODEL), lambda i, e: (e, 0, 0)),
                  pl.BlockSpec((1, D_MODEL), lambda i, e: (0, 0))],
        out_specs=row(D_MODEL),
        out_shape=jax.ShapeDtypeStruct((n, D_MODEL), F32),
        scratch_shapes=[pltpu.VMEM((tm, D_MODEL), F32)],
        compiler_params=_params("parallel", "arbitrary"),
        name="moe",
    )(h, gates, x2, wg_bf, wu_bf, wd_bf, gf)


def _rope_tables(pos):
    half = ROT_DIM // 2
    inv = ROPE_THETA ** (-jnp.arange(half, dtype=F32) * 2.0 / ROT_DIM)
    ang = pos.astype(F32)[:, None] * inv[None, :]
    cos, sin = jnp.cos(ang), jnp.sin(ang)
    n = pos.shape[0]
    one = jnp.ones((n, MOBA_HD - ROT_DIM), F32)
    zero = jnp.zeros((n, MOBA_HD - ROT_DIM), F32)
    zh = jnp.zeros((n, half), F32)
    cos_h = jnp.concatenate([cos, cos, one], axis=1)
    sa_h = jnp.concatenate([-sin, zh, zero], axis=1)
    sb_h = jnp.concatenate([zh, sin, zero], axis=1)
    rep = LANES // MOBA_HD
    return tuple(jnp.concatenate([t] * rep, axis=1) for t in (cos_h, sa_h, sb_h))


def _layer(x, pos, s0, mem_k, mem_v, wts, gla_chunk, moba_fn):
    bsz, L, _ = x.shape
    n = bsz * L
    xf = x.reshape(n, D_MODEL)
    tm = _row_tile(n, 512)
    if L % tm == 0:
        tabs = _rope_tables(pos)
    else:
        tabs = _rope_tables(jnp.tile(pos, n // L))
    gq, gk, gv, gg, la, mq, mk, mv = _inproj(xf, wts["norm_mix"], wts["w_in"], wts["wa_h"], wts["wa_l"], wts["b_a"],
                                             *tabs)
    r3 = lambda a: a.reshape(bsz, L, a.shape[-1])
    og, s_fin = _gla(r3(gq), r3(gk), r3(gv), r3(gg), r3(la), s0.reshape(bsz, GLA_KW, GLA_DV), wts["gla_norm"],
                     gla_chunk)
    mo = moba_fn(r3(mq), r3(mk), r3(mv))
    x1, qm = _outproj(xf, og.reshape(n, GLA_WIDTH), mo.reshape(n, MOBA_WIDTH), wts["w_out"], wts["norm_mem_q"],
                      wts["w_mq"])
    om = _memattn(r3(qm), mem_k, mem_v)
    x2, h3, gates = _memout_router(x1, om.reshape(n, MEM_WIDTH), wts["w_mo"], wts["norm_ffn"], wts["wr_h"],
                                   wts["wr_l"], wts["b_r"])
    y = _moe(h3, gates, x2, wts["w_e_gate"], wts["w_e_up"], wts["w_e_down"], wts["norm_final"])
    return (y.reshape(bsz, L, D_MODEL), mk.reshape(bsz, L, MOBA_HEADS, MOBA_HD), mv.reshape(bsz, L, MOBA_HEADS, MOBA_HD),
            s_fin.reshape(bsz, GLA_HEADS, GLA_DK, GLA_DV))


def _prompt_moba(mq, mk, mv):
    bsz, L, _ = mq.shape
    nb = L // MOBA_BLOCK
    assert L % MOBA_BLOCK == 0 and nb <= MOBA_HD
    means = _blockmeans(mk)
    means_pad = jnp.pad(means, ((0, 0), (0, LANES - nb), (0, 0)))
    qa, ka, va = _moba_prep(mq, mk, mv, means_pad)
    return _moba_attn(qa, ka, va)


def kernel(x_prompt, x_sample, mem_prompt, cache_moba_k, cache_moba_v, state_gla, cache_mem_k, cache_mem_v, page_table, norm_mix, w_in, w_a2, b_a, gla_norm, w_out, norm_mem_q, norm_mem_kv, w_mq, w_mk, w_mv, w_mo, norm_ffn, w_grp, b_grp, w_router, b_router, w_e_gate, w_e_up, w_e_down, norm_final):
    depth = w_in.shape[0]
    assert depth == 1, "single-layer step"
    bp, lp, _ = x_prompt.shape
    bs, ls, _ = x_sample.shape
    page_rows = cache_moba_k.shape[2]
    n_pages = page_table.shape[1]
    past = n_pages * page_rows
    assert past % MOBA_BLOCK == 0 and ls < MOBA_BLOCK and past // MOBA_BLOCK >= MOBA_TOPK
    l = 0

    sizes = (GLA_KW, GLA_KW, GLA_WIDTH, GLA_WIDTH, GLA_RANK, MOBA_WIDTH, MOBA_WIDTH, MOBA_WIDTH)
    offs = np.concatenate([[0], np.cumsum(sizes)])
    w = w_in[l]
    parts = [w[:, offs[i]:offs[i + 1]] for i in range(8)]
    w_re = jnp.concatenate(parts[:4] + parts[5:] + [parts[4], jnp.zeros((D_MODEL, LANES - GLA_RANK), F32)], axis=1)
    wa_pad = jnp.pad(w_a2[l], ((0, LANES - GLA_RANK), (0, 0)))
    wa_h, wa_l = _hi_lo(wa_pad)
    wr = jnp.concatenate([w_grp[l], w_router[l], jnp.zeros((D_MODEL, LANES - N_GROUPS - N_EXPERTS), F32)], axis=1)
    wr_h, wr_l = _hi_lo(wr)
    b_r = jnp.concatenate([b_grp[l], b_router[l], jnp.zeros((LANES - N_GROUPS - N_EXPERTS,), F32)]).reshape(1, LANES)
    wts = dict(
        norm_mix=norm_mix[l].reshape(1, D_MODEL), w_in=w_re.astype(BF16), wa_h=wa_h, wa_l=wa_l,
        b_a=b_a[l].reshape(1, GLA_KW), gla_norm=gla_norm[l].reshape(1, GLA_WIDTH), w_out=w_out[l].astype(BF16),
        norm_mem_q=norm_mem_q[l].reshape(1, D_MODEL), w_mq=w_mq[l].astype(BF16), w_mo=w_mo[l].astype(BF16),
        norm_ffn=norm_ffn[l].reshape(1, D_MODEL), wr_h=wr_h, wr_l=wr_l, b_r=b_r,
        w_e_gate=w_e_gate[l].astype(BF16), w_e_up=w_e_up[l].astype(BF16), w_e_down=w_e_down[l].astype(BF16),
        norm_final=norm_final.reshape(1, D_MODEL),
    )

    memk_p, memv_p = _memkv(mem_prompt, norm_mem_kv[l].reshape(1, D_MODEL), w_mk[l].astype(BF16),
                            w_mv[l].astype(BF16))
    pos_p = jnp.arange(lp, dtype=jnp.int32)
    s0_p = jnp.zeros((bp, GLA_HEADS, GLA_DK, GLA_DV), F32)
    y_p, kp, vp, sp = _layer(x_prompt, pos_p, s0_p, memk_p, memv_p, wts, math.gcd(GLA_CHUNK, lp), _prompt_moba)

    pos_s = past + jnp.arange(ls, dtype=jnp.int32)
    own_np = ((past + np.arange(ls)) // MOBA_BLOCK).astype(np.float32)
    own_rows = jnp.asarray(np.tile(own_np, MOBA_HEADS).reshape(MOBA_HEADS * ls, 1))
    n_pool = cache_moba_k.shape[1]
    ck = cache_moba_k[l].reshape(n_pool, page_rows, MOBA_WIDTH)
    cv = cache_moba_v[l].reshape(n_pool, page_rows, MOBA_WIDTH)
    sample_moba = lambda mq, mk, mv: _smoba(page_table, ck, cv, mq, mk, mv, own_rows)
    m_tok = cache_mem_k.shape[2]
    y_s, ks, vs, ss = _layer(x_sample, pos_s, state_gla[l], cache_mem_k[l].reshape(bs, m_tok, MEM_WIDTH),
                             cache_mem_v[l].reshape(bs, m_tok, MEM_WIDTH), wts, math.gcd(GLA_CHUNK, ls), sample_moba)

    m_p = mem_prompt.shape[1]
    return (y_p, y_s, kp[None], vp[None], sp[None], memk_p.reshape(1, bp, m_p, MEM_HEADS, MEM_HD),
            memv_p.reshape(1, bp, m_p, MEM_HEADS, MEM_HD), ks[None], vs[None], ss[None])
```
